```python
import jax, jax.numpy as jnp
from jax import lax
import numpy as np

D_MODEL = 1024
BATCH = 4
SEQ = 8192
DEPTH = 4

CHUNK = 64
EXPAND = 2
D_MIX = EXPAND * D_MODEL
D_A = D_MIX // 2
D_B = D_MIX - D_A
D_C = D_MIX // 2
D_D = D_MIX - D_C
POOL_WINDOWS = (2, 4, 8, 16)
N_POOL_GROUPS = len(POOL_WINDOWS)
POOL_GROUP = D_A // N_POOL_GROUPS
SHORT_CONV = 3
SGU_BLOCK = 128
SGU_HEADS = 4
SGU_HEAD_DIM = D_C // SGU_HEADS
CONF_CONV = 31
IN_EVEN = 2 * D_A + 4 * D_B
IN_ODD = 3 * D_C + 3 * D_D
N_EVEN = (DEPTH + 1) // 2
N_ODD = DEPTH // 2
DEEPNORM_ALPHA = (2 * DEPTH) ** 0.25
DEEPNORM_BETA = (8 * DEPTH) ** -0.25
LN_EPS = 1e-5

kernel_name = "hybrid_pool_conv_sgu_conformer_deepnorm"


def layer_norm(x, g, b):
    xf = x.astype(jnp.float32)
    mu = jnp.mean(xf, axis=-1, keepdims=True)
    var = jnp.mean(jnp.square(xf - mu), axis=-1, keepdims=True)
    return ((xf - mu) * lax.rsqrt(var + LN_EPS) * g.astype(jnp.float32) + b.astype(jnp.float32)).astype(x.dtype)


def split_cols(z, sizes):
    points = list(np.cumsum(sizes)[:-1])
    return jnp.split(z, points, axis=-1)


def causal_depthwise_conv(z, w, b):
    k = w.shape[0]
    y = lax.conv_general_dilated(
        z, w[:, None, :].astype(z.dtype), window_strides=(1,), padding=[(k - 1, 0)],
        dimension_numbers=('NWC', 'WIO', 'NWC'), feature_group_count=z.shape[-1])
    return y + b.astype(z.dtype)


def multi_scale_pool(z):
    s = z.shape[1]
    zf = z.astype(jnp.float32)
    cs0 = jnp.pad(jnp.cumsum(zf, axis=1), ((0, 0), (1, 0), (0, 0)))
    pos = jnp.arange(1, s + 1, dtype=jnp.float32)
    means = []
    for g, w in enumerate(POOL_WINDOWS):
        cg = cs0[..., g * POOL_GROUP:(g + 1) * POOL_GROUP]
        lagged = jnp.pad(cg[:, :s + 1 - w], ((0, 0), (w - 1, 0), (0, 0)))
        count = jnp.minimum(pos, float(w))[None, :, None]
        means.append((cg[:, 1:] - lagged) / count)
    return (jnp.concatenate(means, axis=-1) - zf).astype(z.dtype)


def sgu_mask():
    idx = jnp.arange(SGU_BLOCK)
    return (idx[None, :] // CHUNK) <= (idx[:, None] // CHUNK)


def pool_conv_layer(x, w_in, w_out, pool_w, pool_scale, sconv_w, sconv_b):
    bsz, s, _ = x.shape
    xa, ga, h, bg, cg, gb = split_cols(x @ w_in, [D_A, D_A, D_B, D_B, D_B, D_B])
    pooled = multi_scale_pool(xa).reshape(bsz, s, N_POOL_GROUPS, POOL_GROUP)
    ya = jnp.einsum('bsgc,gcd->bsgd', pooled, pool_w).reshape(bsz, s, D_A) * pool_scale
    ya = ya * jax.nn.silu(ga)
    yb = bg * causal_depthwise_conv(cg * h, sconv_w, sconv_b)
    yb = yb * jax.nn.silu(gb)
    return jnp.concatenate([ya, yb], axis=-1) @ w_out


def sgu_conformer_layer(x, w_in, w_out, sgu_ln_g, sgu_ln_b, sgu_w, sgu_b,
                        dconv_w, dconv_b, dnorm_g, dnorm_b):
    bsz, s, _ = x.shape
    u, v, gc, a, bglu, gd = split_cols(x @ w_in, [D_C, D_C, D_C, D_D, D_D, D_D])
    v = layer_norm(v, sgu_ln_g, sgu_ln_b)
    v = v.reshape(bsz, s // SGU_BLOCK, SGU_BLOCK, SGU_HEADS, SGU_HEAD_DIM)
    ws = jnp.where(sgu_mask()[None], sgu_w, 0.0).astype(v.dtype)
    sv = jnp.einsum('hij,bnjhc->bnihc', ws, v) + jnp.transpose(sgu_b)[:, :, None].astype(v.dtype)
    yc = u * sv.reshape(bsz, s, D_C) * jax.nn.silu(gc)
    z = a * jax.nn.sigmoid(bglu)
    z = causal_depthwise_conv(z, dconv_w, dconv_b)
    z = jax.nn.silu(layer_norm(z, dnorm_g, dnorm_b))
    yd = z * jax.nn.silu(gd)
    return jnp.concatenate([yc, yd], axis=-1) @ w_out


def setup_inputs(seed: int = 0) -> dict:
    key = jax.random.key(seed)
    ks = jax.random.split(key, 20)
    f32 = jnp.float32
    nrm = lambda k, shp, sc: (jax.random.normal(k, shp, f32) * sc).astype(f32)
    return {
        "x": nrm(ks[0], (BATCH, SEQ, D_MODEL), 1.0),
        "ln_g": 1.0 + nrm(ks[1], (DEPTH, D_MODEL), 0.02),
        "ln_b": nrm(ks[2], (DEPTH, D_MODEL), 0.02),
        "w_in_even": nrm(ks[3], (N_EVEN, D_MODEL, IN_EVEN), D_MODEL ** -0.5),
        "w_out_even": nrm(ks[4], (N_EVEN, D_MIX, D_MODEL), DEEPNORM_BETA * D_MIX ** -0.5),
        "pool_w": nrm(ks[5], (N_EVEN, N_POOL_GROUPS, POOL_GROUP, POOL_GROUP), POOL_GROUP ** -0.5),
        "pool_scale": 1.0 + nrm(ks[6], (N_EVEN, D_A), 0.1),
        "sconv_w": nrm(ks[7], (N_EVEN, SHORT_CONV, D_B), SHORT_CONV ** -0.5),
        "sconv_b": nrm(ks[8], (N_EVEN, D_B), 0.02),
        "w_in_odd": nrm(ks[9], (N_ODD, D_MODEL, IN_ODD), D_MODEL ** -0.5),
        "w_out_odd": nrm(ks[10], (N_ODD, D_MIX, D_MODEL), DEEPNORM_BETA * D_MIX ** -0.5),
        "sgu_ln_g": 1.0 + nrm(ks[11], (N_ODD, D_C), 0.02),
        "sgu_ln_b": nrm(ks[12], (N_ODD, D_C), 0.02),
        "sgu_w": nrm(ks[13], (N_ODD, SGU_HEADS, SGU_BLOCK, SGU_BLOCK), SGU_BLOCK ** -0.5),
        "sgu_b": 1.0 + nrm(ks[14], (N_ODD, SGU_HEADS, SGU_BLOCK), 0.01),
        "dconv_w": nrm(ks[15], (N_ODD, CONF_CONV, D_D), CONF_CONV ** -0.5),
        "dconv_b": nrm(ks[16], (N_ODD, D_D), 0.02),
        "dnorm_g": 1.0 + nrm(ks[17], (N_ODD, D_D), 0.02),
        "dnorm_b": nrm(ks[18], (N_ODD, D_D), 0.02),
    }


def reference(x, ln_g, ln_b, w_in_even, w_out_even, pool_w, pool_scale, sconv_w, sconv_b,
              w_in_odd, w_out_odd, sgu_ln_g, sgu_ln_b, sgu_w, sgu_b,
              dconv_w, dconv_b, dnorm_g, dnorm_b):
    for layer in range(DEPTH):
        i = layer // 2
        if layer % 2 == 0:
            y = pool_conv_layer(x, w_in_even[i], w_out_even[i], pool_w[i], pool_scale[i],
                                sconv_w[i], sconv_b[i])
        else:
            y = sgu_conformer_layer(x, w_in_odd[i], w_out_odd[i], sgu_ln_g[i], sgu_ln_b[i],
                                    sgu_w[i], sgu_b[i], dconv_w[i], dconv_b[i],
                                    dnorm_g[i], dnorm_b[i])
        x = layer_norm(DEEPNORM_ALPHA * x + y, ln_g[layer], ln_b[layer])
    return x
```

```python
import functools

import jax
import jax.numpy as jnp
from jax.experimental import pallas as pl
from jax.experimental.pallas import tpu as pltpu

D_MODEL = 1024
D_HALF = 1024
POOL_WINDOWS = (2, 4, 8, 16)
POOL_GROUP = 256
SHORT_CONV = 3
SGU_BLOCK = 128
SGU_CHUNK = 64
SGU_HEADS = 4
SGU_HEAD_DIM = 256
CONF_CONV = 31
DEPTH = 4
DEEPNORM_ALPHA = (2 * DEPTH) ** 0.25
LN_EPS = 1e-5

SEQ_TILE = 512
COL_CHUNK = 256
POOL_HALO = 16
SCONV_HALO = 8
DCONV_HALO = 32
VMEM_LIMIT_BYTES = 56 * 1024 * 1024

_F32 = jnp.float32
_BF16 = jnp.bfloat16


def _dot(a, b):
    return jnp.dot(a, b, preferred_element_type=_F32)


def _layer_norm(v, g, b):
    mu = jnp.mean(v, axis=-1, keepdims=True)
    d = v - mu
    var = jnp.mean(d * d, axis=-1, keepdims=True)
    return d * jax.lax.rsqrt(var + LN_EPS) * g + b


def _silu(v):
    return v * jax.nn.sigmoid(v)


def _even_kernel(x_ref, win_ref, wout_ref, poolw_ref, pscale_ref, scw_ref, scb_ref,
                 g_ref, b_ref, o_ref, xa_buf, ch_buf, y_buf):
    ts = SEQ_TILE
    j = pl.program_id(1)

    @pl.when(j == 0)
    def _():
        xa_buf[0:POOL_HALO, :] = jnp.zeros((POOL_HALO, D_HALF), _F32)
        ch_buf[0:SCONV_HALO, :] = jnp.zeros((SCONV_HALO, D_HALF), _F32)

    x = x_ref[...]
    xb = x.astype(_BF16)
    pos = (j * ts + 1 + jax.lax.broadcasted_iota(jnp.int32, (ts, 1), 0)).astype(_F32)

    for g, w in enumerate(POOL_WINDOWS):
        c0 = g * POOL_GROUP
        cols = slice(c0, c0 + POOL_GROUP)
        xa = _dot(xb, win_ref[:, c0:c0 + POOL_GROUP])
        ga = _dot(xb, win_ref[:, D_HALF + c0:D_HALF + c0 + POOL_GROUP])
        xa_buf[POOL_HALO:POOL_HALO + ts, cols] = xa
        wsum = xa
        for k in range(1, w):
            wsum = wsum + xa_buf[POOL_HALO - k:POOL_HALO - k + ts, cols]
        xa_buf[0:POOL_HALO, cols] = xa_buf[ts:ts + POOL_HALO, cols]
        pooled = wsum / jnp.minimum(pos, float(w)) - xa
        ya = _dot(pooled.astype(_BF16), poolw_ref[g]) * pscale_ref[:, cols] * _silu(ga)
        y_buf[:, cols] = ya.astype(_BF16)

    for c in range(D_HALF // COL_CHUNK):
        c0 = c * COL_CHUNK
        cols = slice(c0, c0 + COL_CHUNK)
        h = _dot(xb, win_ref[:, 2 * D_HALF + c0:2 * D_HALF + c0 + COL_CHUNK])
        bg = _dot(xb, win_ref[:, 3 * D_HALF + c0:3 * D_HALF + c0 + COL_CHUNK])
        cg = _dot(xb, win_ref[:, 4 * D_HALF + c0:4 * D_HALF + c0 + COL_CHUNK])
        gb = _dot(xb, win_ref[:, 5 * D_HALF + c0:5 * D_HALF + c0 + COL_CHUNK])
        ch = cg * h
        ch_buf[SCONV_HALO:SCONV_HALO + ts, cols] = ch
        conv = ch * scw_ref[SHORT_CONV - 1:SHORT_CONV, cols] + scb_ref[:, cols]
        for k in range(1, SHORT_CONV):
            conv = conv + (ch_buf[SCONV_HALO - k:SCONV_HALO - k + ts, cols]
                           * scw_ref[SHORT_CONV - 1 - k:SHORT_CONV - k, cols])
        ch_buf[0:SCONV_HALO, cols] = ch_buf[ts:ts + SCONV_HALO, cols]
        yb = bg * conv * _silu(gb)
        y_buf[:, D_HALF + c0:D_HALF + c0 + COL_CHUNK] = yb.astype(_BF16)

    y = _dot(y_buf[...], wout_ref[...])
    o_ref[...] = _layer_norm(DEEPNORM_ALPHA * x + y, g_ref[...], b_ref[...])


def _odd_kernel(x_ref, win_ref, wout_ref, sg_ref, sb_ref, sw_ref, sbias_ref,
                dw_ref, db_ref, dg_ref, dbeta_ref, g_ref, b_ref, o_ref,
                z_buf, cv_buf, y_buf):
    ts = SEQ_TILE
    j = pl.program_id(1)

    @pl.when(j == 0)
    def _():
        z_buf[0:DCONV_HALO, :] = jnp.zeros((DCONV_HALO, D_HALF), _F32)

    x = x_ref[...]
    xb = x.astype(_BF16)

    v = _dot(xb, win_ref[:, D_HALF:2 * D_HALF])
    vn = _layer_norm(v, sg_ref[...], sb_ref[...]).astype(_BF16)
    row = jax.lax.broadcasted_iota(jnp.int32, (SGU_BLOCK, SGU_BLOCK), 0)
    col = jax.lax.broadcasted_iota(jnp.int32, (SGU_BLOCK, SGU_BLOCK), 1)
    mask = (col // SGU_CHUNK) <= (row // SGU_CHUNK)
    for hd in range(SGU_HEADS):
        c0 = hd * SGU_HEAD_DIM
        cols = slice(c0, c0 + SGU_HEAD_DIM)
        ws = jnp.where(mask, sw_ref[hd], 0.0).astype(_BF16)
        u = _dot(xb, win_ref[:, c0:c0 + SGU_HEAD_DIM])
        gc = _dot(xb, win_ref[:, 2 * D_HALF + c0:2 * D_HALF + c0 + SGU_HEAD_DIM])
        gate = u * _silu(gc)
        bias = sbias_ref[:, hd:hd + 1]
        for n in range(ts // SGU_BLOCK):
            r0 = n * SGU_BLOCK
            sv = _dot(ws, vn[r0:r0 + SGU_BLOCK, cols]) + bias
            y_buf[r0:r0 + SGU_BLOCK, cols] = (gate[r0:r0 + SGU_BLOCK] * sv).astype(_BF16)

    for c in range(D_HALF // COL_CHUNK):
        c0 = c * COL_CHUNK
        cols = slice(c0, c0 + COL_CHUNK)
        a = _dot(xb, win_ref[:, 3 * D_HALF + c0:3 * D_HALF + c0 + COL_CHUNK])
        bglu = _dot(xb, win_ref[:, 4 * D_HALF + c0:4 * D_HALF + c0 + COL_CHUNK])
        z = a * jax.nn.sigmoid(bglu)
        z_buf[DCONV_HALO:DCONV_HALO + ts, cols] = z
        conv = z * dw_ref[CONF_CONV - 1:CONF_CONV, cols] + db_ref[:, cols]
        for k in range(1, CONF_CONV):
            conv = conv + (z_buf[DCONV_HALO - k:DCONV_HALO - k + ts, cols]
                           * dw_ref[CONF_CONV - 1 - k:CONF_CONV - k, cols])
        z_buf[0:DCONV_HALO, cols] = z_buf[ts:ts + DCONV_HALO, cols]
        cv_buf[:, cols] = conv
    zn = _silu(_layer_norm(cv_buf[...], dg_ref[...], dbeta_ref[...]))
    for c in range(D_HALF // COL_CHUNK):
        c0 = c * COL_CHUNK
        gd = _dot(xb, win_ref[:, 5 * D_HALF + c0:5 * D_HALF + c0 + COL_CHUNK])
        y_buf[:, D_HALF + c0:D_HALF + c0 + COL_CHUNK] = (
            zn[:, c0:c0 + COL_CHUNK] * _silu(gd)).astype(_BF16)

    y = _dot(y_buf[...], wout_ref[...])
    o_ref[...] = _layer_norm(DEEPNORM_ALPHA * x + y, g_ref[...], b_ref[...])


def _resident(shape):
    zeros = (0,) * len(shape)
    return pl.BlockSpec(shape, lambda b, j: zeros, pipeline_mode=pl.Buffered(1))


def _layer_call(body, x, params, scratch_shapes, name):
    bsz, seq, d = x.shape
    x_spec = pl.BlockSpec((None, SEQ_TILE, d), lambda b, j: (b, j, 0))
    return pl.pallas_call(
        body,
        grid=(bsz, seq // SEQ_TILE),
        in_specs=[x_spec] + [_resident(p.shape) for p in params],
        out_specs=x_spec,
        out_shape=jax.ShapeDtypeStruct(x.shape, x.dtype),
        scratch_shapes=scratch_shapes,
        compiler_params=pltpu.CompilerParams(
            dimension_semantics=("arbitrary", "arbitrary"),
            vmem_limit_bytes=VMEM_LIMIT_BYTES),
        name=name,
    )(x, *params)


def _even_layer(x, w_in, w_out, pool_w, pool_scale, sconv_w, sconv_b, ln_g, ln_b):
    params = (w_in.astype(_BF16), w_out.astype(_BF16), pool_w.astype(_BF16),
              pool_scale[None, :], sconv_w, sconv_b[None, :], ln_g[None, :], ln_b[None, :])
    scratch = [pltpu.VMEM((SEQ_TILE + POOL_HALO, D_HALF), _F32),
               pltpu.VMEM((SEQ_TILE + SCONV_HALO, D_HALF), _F32),
               pltpu.VMEM((SEQ_TILE, 2 * D_HALF), _BF16)]
    return _layer_call(_even_kernel, x, params, scratch, "pool_conv_layer")


def _odd_layer(x, w_in, w_out, sgu_ln_g, sgu_ln_b, sgu_w, sgu_b, dconv_w, dconv_b,
               dnorm_g, dnorm_b, ln_g, ln_b):
    params = (w_in.astype(_BF16), w_out.astype(_BF16), sgu_ln_g[None, :], sgu_ln_b[None, :],
              sgu_w, jnp.transpose(sgu_b), dconv_w, dconv_b[None, :],
              dnorm_g[None, :], dnorm_b[None, :], ln_g[None, :], ln_b[None, :])
    scratch = [pltpu.VMEM((SEQ_TILE + DCONV_HALO, D_HALF), _F32),
               pltpu.VMEM((SEQ_TILE, D_HALF), _F32),
               pltpu.VMEM((SEQ_TILE, 2 * D_HALF), _BF16)]
    return _layer_call(_odd_kernel, x, params, scratch, "sgu_conformer_layer")


def kernel(x, ln_g, ln_b, w_in_even, w_out_even, pool_w, pool_scale, sconv_w, sconv_b,
           w_in_odd, w_out_odd, sgu_ln_g, sgu_ln_b, sgu_w, sgu_b,
           dconv_w, dconv_b, dnorm_g, dnorm_b):
    assert x.shape[1] % SEQ_TILE == 0 and x.shape[2] == D_MODEL
    for layer in range(DEPTH):
        i = layer // 2
        if layer % 2 == 0:
            x = _even_layer(x, w_in_even[i], w_out_even[i], pool_w[i], pool_scale[i],
                            sconv_w[i], sconv_b[i], ln_g[layer], ln_b[layer])
        else:
            x = _odd_layer(x, w_in_odd[i], w_out_odd[i], sgu_ln_g[i], sgu_ln_b[i],
                           sgu_w[i], sgu_b[i], dconv_w[i], dconv_b[i],
                           dnorm_g[i], dnorm_b[i], ln_g[layer], ln_b[layer])
    return x
```

```python
import jax
import jax.numpy as jnp
from jax.experimental import pallas as pl
from jax.experimental.pallas import tpu as pltpu

D_MODEL = 1024
D_HALF = 1024
POOL_WINDOWS = (2, 4, 8, 16)
POOL_GROUP = 256
SHORT_CONV = 3
SGU_BLOCK = 128
SGU_CHUNK = 64
SGU_HEADS = 4
SGU_HEAD_DIM = 256
CONF_CONV = 31
DEPTH = 4
DEEPNORM_ALPHA = (2 * DEPTH) ** 0.25
LN_EPS = 1e-5

_SUBLANES = 8
SEQ_TILE = 512
COL_CHUNK = 256
POOL_HALO = 16
SCONV_HALO = 8
DCONV_HALO = 32
VMEM_LIMIT_BYTES = 56 * 1024 * 1024

_F32 = jnp.float32
_BF16 = jnp.bfloat16


def _dot(a, b):
    return jnp.dot(a, b, preferred_element_type=_F32)


def _layer_norm(v, g, b):
    mu = jnp.mean(v, axis=-1, keepdims=True)
    d = v - mu
    var = jnp.mean(d * d, axis=-1, keepdims=True)
    return d * jax.lax.rsqrt(var + LN_EPS) * g + b


def _silu(v):
    return v * jax.nn.sigmoid(v)


def _even_kernel(x_ref, win_ref, wout_ref, poolw_ref, pscale_ref, scw_ref, scb_ref,
                 g_ref, b_ref, o_ref, xa_buf, ch_buf, y_buf):
    ts = SEQ_TILE
    j = pl.program_id(1)

    @pl.when(j == 0)
    def _():
        xa_buf[...] = jnp.zeros((POOL_HALO, D_HALF), _F32)
        ch_buf[...] = jnp.zeros((SCONV_HALO, D_HALF), _F32)

    x = x_ref[...]
    xb = x.astype(_BF16)
    pos = (j * ts + 1 + jax.lax.broadcasted_iota(jnp.int32, (ts, 1), 0)).astype(_F32)

    def proj(block, c0):
        lo = block * D_HALF + c0
        return _dot(xb, win_ref[:, lo:lo + COL_CHUNK])

    for g, w in enumerate(POOL_WINDOWS):
        c0 = g * POOL_GROUP
        cols = slice(c0, c0 + POOL_GROUP)
        xa = proj(0, c0)
        ga = proj(1, c0)
        wsum = jnp.concatenate([xa_buf[:, cols], xa], axis=0)
        xa_buf[:, cols] = xa[ts - POOL_HALO:ts]
        shift = 1
        while shift < w:
            wsum = wsum + pltpu.roll(wsum, shift, axis=0)
            shift *= 2
        pooled = wsum[POOL_HALO:] / jnp.minimum(pos, float(w)) - xa
        ya = _dot(pooled.astype(_BF16), poolw_ref[g]) * pscale_ref[:, cols] * _silu(ga)
        y_buf[:, cols] = ya.astype(_BF16)
    y = _dot(y_buf[:, 0:D_HALF], wout_ref[0:D_HALF, :])

    for c in range(D_HALF // COL_CHUNK):
        c0 = c * COL_CHUNK
        cols = slice(c0, c0 + COL_CHUNK)
        ch = proj(4, c0) * proj(2, c0)
        ext = jnp.concatenate([ch_buf[:, cols], ch], axis=0)
        ch_buf[:, cols] = ch[ts - SCONV_HALO:ts]
        conv = ext * scw_ref[0:1, cols]
        for k in range(1, SHORT_CONV):
            conv = ext * scw_ref[k:k + 1, cols] + pltpu.roll(conv, 1, axis=0)
        conv = conv[SCONV_HALO:] + scb_ref[:, cols]
        yb = proj(3, c0) * conv * _silu(proj(5, c0))
        y_buf[:, D_HALF + c0:D_HALF + c0 + COL_CHUNK] = yb.astype(_BF16)
    y = y + _dot(y_buf[:, D_HALF:2 * D_HALF], wout_ref[D_HALF:2 * D_HALF, :])

    o_ref[...] = _layer_norm(DEEPNORM_ALPHA * x + y, g_ref[...], b_ref[...])


def _conv31_chunk(ext, dw_ref, cols):
    rows = SEQ_TILE + _SUBLANES
    conv = None
    for r in range(_SUBLANES - 1, -1, -1):
        part = None
        for q in range((CONF_CONV - 1 - r) // _SUBLANES + 1):
            d = _SUBLANES * q + r
            start = DCONV_HALO - _SUBLANES - _SUBLANES * q
            term = ext[start:start + rows] * dw_ref[CONF_CONV - 1 - d:CONF_CONV - d, cols]
            part = term if part is None else part + term
        conv = part if conv is None else part + pltpu.roll(conv, 1, axis=0)
    return conv


def _odd_kernel(x_ref, win_ref, wout_ref, sg_ref, sb_ref, sw_ref, sbias_ref,
                dw_ref, db_ref, dg_ref, dbeta_ref, g_ref, b_ref, o_ref,
                z_buf, cv_buf, v_buf, gate_buf, sgd_buf, y_buf):
    ts = SEQ_TILE
    j = pl.program_id(1)

    @pl.when(j == 0)
    def _():
        z_buf[...] = jnp.zeros((DCONV_HALO, D_HALF), _F32)

    x = x_ref[...]
    xb = x.astype(_BF16)

    def proj(block, c0):
        lo = block * D_HALF + c0
        return _dot(xb, win_ref[:, lo:lo + COL_CHUNK])

    for c in range(D_HALF // COL_CHUNK):
        c0 = c * COL_CHUNK
        cols = slice(c0, c0 + COL_CHUNK)
        z = proj(3, c0) * jax.nn.sigmoid(proj(4, c0))
        ext = jnp.concatenate([z_buf[:, cols], z], axis=0)
        z_buf[:, cols] = z[ts - DCONV_HALO:ts]
        v_buf[:, cols] = proj(1, c0)
        gate_buf[:, cols] = proj(0, c0) * _silu(proj(2, c0))
        sgd_buf[:, cols] = _silu(proj(5, c0))
        cv_buf[:, cols] = _conv31_chunk(ext, dw_ref, cols)[_SUBLANES:] + db_ref[:, cols]

    vn = _layer_norm(v_buf[...], sg_ref[...], sb_ref[...]).astype(_BF16)
    row = jax.lax.broadcasted_iota(jnp.int32, (SGU_BLOCK, SGU_BLOCK), 0)
    col = jax.lax.broadcasted_iota(jnp.int32, (SGU_BLOCK, SGU_BLOCK), 1)
    mask = (col // SGU_CHUNK) <= (row // SGU_CHUNK)
    for hd in range(SGU_HEADS):
        c0 = hd * SGU_HEAD_DIM
        cols = slice(c0, c0 + SGU_HEAD_DIM)
        ws = jnp.where(mask, sw_ref[hd], 0.0).astype(_BF16)
        bias = sbias_ref[:, hd:hd + 1]
        for n in range(ts // SGU_BLOCK):
            rws = slice(n * SGU_BLOCK, (n + 1) * SGU_BLOCK)
            sv = _dot(ws, vn[rws, cols]) + bias
            y_buf[rws, cols] = (gate_buf[rws, cols] * sv).astype(_BF16)
    y = _dot(y_buf[:, 0:D_HALF], wout_ref[0:D_HALF, :])

    zn = _silu(_layer_norm(cv_buf[...], dg_ref[...], dbeta_ref[...]))
    y_buf[:, D_HALF:2 * D_HALF] = (zn * sgd_buf[...]).astype(_BF16)
    y = y + _dot(y_buf[:, D_HALF:2 * D_HALF], wout_ref[D_HALF:2 * D_HALF, :])

    o_ref[...] = _layer_norm(DEEPNORM_ALPHA * x + y, g_ref[...], b_ref[...])


def _resident(shape):
    zeros = (0,) * len(shape)
    return pl.BlockSpec(shape, lambda b, j: zeros, pipeline_mode=pl.Buffered(1))


def _layer_call(body, x, params, scratch_shapes, name):
    bsz, seq, d = x.shape
    x_spec = pl.BlockSpec((None, SEQ_TILE, d), lambda b, j: (b, j, 0))
    return pl.pallas_call(
        body,
        grid=(bsz, seq // SEQ_TILE),
        in_specs=[x_spec] + [_resident(p.shape) for p in params],
        out_specs=x_spec,
        out_shape=jax.ShapeDtypeStruct(x.shape, x.dtype),
        scratch_shapes=scratch_shapes,
        compiler_params=pltpu.CompilerParams(
            dimension_semantics=("arbitrary", "arbitrary"),
            vmem_limit_bytes=VMEM_LIMIT_BYTES),
        name=name,
    )(x, *params)


def _even_layer(x, w_in, w_out, pool_w, pool_scale, sconv_w, sconv_b, ln_g, ln_b):
    params = (w_in.astype(_BF16), w_out.astype(_BF16), pool_w.astype(_BF16),
              pool_scale[None, :], sconv_w, sconv_b[None, :], ln_g[None, :], ln_b[None, :])
    scratch = [pltpu.VMEM((POOL_HALO, D_HALF), _F32),
               pltpu.VMEM((SCONV_HALO, D_HALF), _F32),
               pltpu.VMEM((SEQ_TILE, 2 * D_HALF), _BF16)]
    return _layer_call(_even_kernel, x, params, scratch, "pool_conv_layer")


def _odd_layer(x, w_in, w_out, sgu_ln_g, sgu_ln_b, sgu_w, sgu_b, dconv_w, dconv_b,
               dnorm_g, dnorm_b, ln_g, ln_b):
    params = (w_in.astype(_BF16), w_out.astype(_BF16), sgu_ln_g[None, :], sgu_ln_b[None, :],
              sgu_w, jnp.transpose(sgu_b), dconv_w, dconv_b[None, :],
              dnorm_g[None, :], dnorm_b[None, :], ln_g[None, :], ln_b[None, :])
    tile_f32 = pltpu.VMEM((SEQ_TILE, D_HALF), _F32)
    scratch = [pltpu.VMEM((DCONV_HALO, D_HALF), _F32),
               tile_f32, tile_f32, tile_f32, tile_f32,
               pltpu.VMEM((SEQ_TILE, 2 * D_HALF), _BF16)]
    return _layer_call(_odd_kernel, x, params, scratch, "sgu_conformer_layer")


def kernel(x, ln_g, ln_b, w_in_even, w_out_even, pool_w, pool_scale, sconv_w, sconv_b,
           w_in_odd, w_out_odd, sgu_ln_g, sgu_ln_b, sgu_w, sgu_b,
           dconv_w, dconv_b, dnorm_g, dnorm_b):
    assert x.shape[1] % SEQ_TILE == 0 and x.shape[2] == D_MODEL
    for layer in range(DEPTH):
        i = layer // 2
        if layer % 2 == 0:
            x = _even_layer(x, w_in_even[i], w_out_even[i], pool_w[i], pool_scale[i],
                            sconv_w[i], sconv_b[i], ln_g[layer], ln_b[layer])
        else:
            x = _odd_layer(x, w_in_odd[i], w_out_odd[i], sgu_ln_g[i], sgu_ln_b[i],
                           sgu_w[i], sgu_b[i], dconv_w[i], dconv_b[i],
                           dnorm_g[i], dnorm_b[i], ln_g[layer], ln_b[layer])
    return x
```

```python
import functools

import jax
import jax.numpy as jnp
from jax.experimental import pallas as pl
from jax.experimental.pallas import tpu as pltpu

D_MODEL = 1024
D_HALF = 1024
POOL_WINDOWS = (2, 4, 8, 16)
POOL_GROUP = 256
SHORT_CONV = 3
SGU_BLOCK = 128
SGU_CHUNK = 64
SGU_HEADS = 4
SGU_HEAD_DIM = 256
CONF_CONV = 31
DEPTH = 4
DEEPNORM_ALPHA = (2 * DEPTH) ** 0.25
LN_EPS = 1e-5

_SUBLANES = 8
EVEN_TILE = 1024
ODD_TILE = 512
COL_CHUNK = 256
POOL_HALO = 16
SCONV_HALO = 8
DCONV_HALO = 32
VMEM_LIMIT_BYTES = 56 * 1024 * 1024

_F32 = jnp.float32
_BF16 = jnp.bfloat16


def _dot(a, b):
    return jnp.dot(a, b, preferred_element_type=_F32)


def _layer_norm(v, g, b):
    mu = jnp.mean(v, axis=-1, keepdims=True)
    d = v - mu
    var = jnp.mean(d * d, axis=-1, keepdims=True)
    return d * jax.lax.rsqrt(var + LN_EPS) * g + b


def _silu(v):
    return v * jax.nn.sigmoid(v)


def _zero(ref):
    ref[...] = jnp.zeros(ref.shape, ref.dtype)


def _even_kernel(ts, tiles_per_seq, x_ref, win_ref, wout_ref, poolw_ref, pscale_ref, scw_ref,
                 scb_ref, g_ref, b_ref, o_ref, xa_buf, ch_buf, y_buf):
    tile_in_seq = pl.program_id(0) % tiles_per_seq

    @pl.when(tile_in_seq == 0)
    def _():
        _zero(xa_buf)
        _zero(ch_buf)

    x = x_ref[...]
    xb = x.astype(_BF16)
    pos = (tile_in_seq * ts + 1 + jax.lax.broadcasted_iota(jnp.int32, (ts, 1), 0)).astype(_F32)

    def proj(block, c0):
        lo = block * D_HALF + c0
        return _dot(xb, win_ref[:, lo:lo + COL_CHUNK])

    for g, w in enumerate(POOL_WINDOWS):
        c0 = g * POOL_GROUP
        cols = slice(c0, c0 + POOL_GROUP)
        xa = proj(0, c0)
        ga = proj(1, c0)
        wsum = jnp.concatenate([xa_buf[:, cols], xa], axis=0)
        xa_buf[:, cols] = xa[ts - POOL_HALO:ts]
        shift = 1
        while shift < w:
            wsum = wsum + pltpu.roll(wsum, shift, axis=0)
            shift *= 2
        pooled = wsum[POOL_HALO:] / jnp.minimum(pos, float(w)) - xa
        ya = _dot(pooled.astype(_BF16), poolw_ref[g]) * pscale_ref[:, cols] * _silu(ga)
        y_buf[:, cols] = ya.astype(_BF16)
    y = _dot(y_buf[:, 0:D_HALF], wout_ref[0:D_HALF, :])

    for c in range(D_HALF // COL_CHUNK):
        c0 = c * COL_CHUNK
        cols = slice(c0, c0 + COL_CHUNK)
        ch = proj(4, c0) * proj(2, c0)
        ext = jnp.concatenate([ch_buf[:, cols], ch], axis=0)
        ch_buf[:, cols] = ch[ts - SCONV_HALO:ts]
        conv = ext * scw_ref[0:1, cols]
        for k in range(1, SHORT_CONV):
            conv = ext * scw_ref[k:k + 1, cols] + pltpu.roll(conv, 1, axis=0)
        conv = conv[SCONV_HALO:] + scb_ref[:, cols]
        yb = proj(3, c0) * conv * _silu(proj(5, c0))
        y_buf[:, D_HALF + c0:D_HALF + c0 + COL_CHUNK] = yb.astype(_BF16)
    y = y + _dot(y_buf[:, D_HALF:2 * D_HALF], wout_ref[D_HALF:2 * D_HALF, :])

    o_ref[...] = _layer_norm(DEEPNORM_ALPHA * x + y, g_ref[...], b_ref[...])


def _conv31_chunk(ext, dw_ref, cols):
    rows = ext.shape[0] - DCONV_HALO + _SUBLANES
    conv = None
    for r in range(_SUBLANES - 1, -1, -1):
        part = None
        for q in range((CONF_CONV - 1 - r) // _SUBLANES + 1):
            d = _SUBLANES * q + r
            start = DCONV_HALO - _SUBLANES - _SUBLANES * q
            term = ext[start:start + rows] * dw_ref[CONF_CONV - 1 - d:CONF_CONV - d, cols]
            part = term if part is None else part + term
        conv = part if conv is None else part + pltpu.roll(conv, 1, axis=0)
    return conv


def _odd_kernel(ts, tiles_per_seq, x_ref, win_ref, wout_ref, sg_ref, sb_ref, sw_ref, sbias_ref,
                dw_ref, db_ref, dg_ref, dbeta_ref, g_ref, b_ref, o_ref,
                z_buf, cv_buf, v_buf, gate_buf, sgd_buf, y_buf):
    @pl.when(pl.program_id(0) % tiles_per_seq == 0)
    def _():
        _zero(z_buf)

    x = x_ref[...]
    xb = x.astype(_BF16)

    def proj(block, c0):
        lo = block * D_HALF + c0
        return _dot(xb, win_ref[:, lo:lo + COL_CHUNK])

    for c in range(D_HALF // COL_CHUNK):
        c0 = c * COL_CHUNK
        cols = slice(c0, c0 + COL_CHUNK)
        z = proj(3, c0) * jax.nn.sigmoid(proj(4, c0))
        ext = jnp.concatenate([z_buf[:, cols], z], axis=0)
        z_buf[:, cols] = z[ts - DCONV_HALO:ts]
        v_buf[:, cols] = proj(1, c0)
        gate_buf[:, cols] = proj(0, c0) * _silu(proj(2, c0))
        sgd_buf[:, cols] = _silu(proj(5, c0))
        cv_buf[:, cols] = _conv31_chunk(ext, dw_ref, cols)[_SUBLANES:] + db_ref[:, cols]

    vn = _layer_norm(v_buf[...], sg_ref[...], sb_ref[...]).astype(_BF16)
    row = jax.lax.broadcasted_iota(jnp.int32, (SGU_BLOCK, SGU_BLOCK), 0)
    col = jax.lax.broadcasted_iota(jnp.int32, (SGU_BLOCK, SGU_BLOCK), 1)
    mask = (col // SGU_CHUNK) <= (row // SGU_CHUNK)
    for hd in range(SGU_HEADS):
        c0 = hd * SGU_HEAD_DIM
        cols = slice(c0, c0 + SGU_HEAD_DIM)
        ws = jnp.where(mask, sw_ref[hd], 0.0).astype(_BF16)
        bias = sbias_ref[:, hd:hd + 1]
        for n in range(ts // SGU_BLOCK):
            rws = slice(n * SGU_BLOCK, (n + 1) * SGU_BLOCK)
            sv = _dot(ws, vn[rws, cols]) + bias
            y_buf[rws, cols] = (gate_buf[rws, cols] * sv).astype(_BF16)
    y = _dot(y_buf[:, 0:D_HALF], wout_ref[0:D_HALF, :])

    zn = _silu(_layer_norm(cv_buf[...], dg_ref[...], dbeta_ref[...]))
    y_buf[:, D_HALF:2 * D_HALF] = (zn * sgd_buf[...]).astype(_BF16)
    y = y + _dot(y_buf[:, D_HALF:2 * D_HALF], wout_ref[D_HALF:2 * D_HALF, :])

    o_ref[...] = _layer_norm(DEEPNORM_ALPHA * x + y, g_ref[...], b_ref[...])


def _layer_param(stacked, layer):
    tail = (0,) * (stacked.ndim - 1)
    return pl.BlockSpec((None,) + stacked.shape[1:], lambda s: (layer,) + tail,
                        pipeline_mode=pl.Buffered(1))


def _layer_call(body, ts, x2d, seq, params, scratch_shapes, name):
    tokens, d = x2d.shape
    tile_spec = pl.BlockSpec((ts, d), lambda s: (s, 0))
    return pl.pallas_call(
        functools.partial(body, ts, seq // ts),
        grid=(tokens // ts,),
        in_specs=[tile_spec] + [_layer_param(p, layer) for p, layer in params],
        out_specs=tile_spec,
        out_shape=jax.ShapeDtypeStruct(x2d.shape, x2d.dtype),
        scratch_shapes=scratch_shapes,
        compiler_params=pltpu.CompilerParams(
            dimension_semantics=("arbitrary",),
            vmem_limit_bytes=VMEM_LIMIT_BYTES),
        name=name,
    )(x2d, *[p for p, _ in params])


def _row(stacked):
    return stacked[:, None, :]


def kernel(x, ln_g, ln_b, w_in_even, w_out_even, pool_w, pool_scale, sconv_w, sconv_b,
           w_in_odd, w_out_odd, sgu_ln_g, sgu_ln_b, sgu_w, sgu_b,
           dconv_w, dconv_b, dnorm_g, dnorm_b):
    bsz, seq, d = x.shape
    assert seq % EVEN_TILE == 0 and seq % ODD_TILE == 0 and d == D_MODEL

    odd_f32 = pltpu.VMEM((ODD_TILE, D_HALF), _F32)
    even_scratch = [pltpu.VMEM((POOL_HALO, D_HALF), _F32), pltpu.VMEM((SCONV_HALO, D_HALF), _F32),
                    pltpu.VMEM((EVEN_TILE, 2 * D_HALF), _BF16)]
    odd_scratch = [pltpu.VMEM((DCONV_HALO, D_HALF), _F32),
                   odd_f32, odd_f32, odd_f32, odd_f32,
                   pltpu.VMEM((ODD_TILE, 2 * D_HALF), _BF16)]

    w_in_even, w_out_even, pool_w, w_in_odd, w_out_odd = (
        w.astype(_BF16) for w in (w_in_even, w_out_even, pool_w, w_in_odd, w_out_odd))
    ln_g, ln_b = _row(ln_g), _row(ln_b)
    pool_scale, sconv_b = _row(pool_scale), _row(sconv_b)
    sgu_ln_g, sgu_ln_b = _row(sgu_ln_g), _row(sgu_ln_b)
    dconv_b, dnorm_g, dnorm_b = _row(dconv_b), _row(dnorm_g), _row(dnorm_b)
    sgu_bias = jnp.transpose(sgu_b, (0, 2, 1))

    h = x.reshape(bsz * seq, d)
    for layer in range(DEPTH):
        i = layer // 2
        if layer % 2 == 0:
            params = [(w_in_even, i), (w_out_even, i), (pool_w, i), (pool_scale, i),
                      (sconv_w, i), (sconv_b, i), (ln_g, layer), (ln_b, layer)]
            h = _layer_call(_even_kernel, EVEN_TILE, h, seq, params, even_scratch,
                            "pool_conv_layer")
        else:
            params = [(w_in_odd, i), (w_out_odd, i), (sgu_ln_g, i), (sgu_ln_b, i),
                      (sgu_w, i), (sgu_bias, i), (dconv_w, i), (dconv_b, i),
                      (dnorm_g, i), (dnorm_b, i), (ln_g, layer), (ln_b, layer)]
            h = _layer_call(_odd_kernel, ODD_TILE, h, seq, params, odd_scratch,
                            "sgu_conformer_layer")
    return h.reshape(bsz, seq, d)
```

```python
import functools

import jax
import jax.numpy as jnp
from jax.experimental import pallas as pl
from jax.experimental.pallas import tpu as pltpu

D_MODEL = 1024
D_HALF = 1024
POOL_WINDOWS = (2, 4, 8, 16)
POOL_GROUP = 256
SHORT_CONV = 3
SGU_BLOCK = 128
SGU_CHUNK = 64
SGU_HEADS = 4
SGU_HEAD_DIM = 256
CONF_CONV = 31
DEPTH = 4
DEEPNORM_ALPHA = (2 * DEPTH) ** 0.25
LN_EPS = 1e-5

_SUBLANES = 8
EVEN_TILE = 1024
ODD_TILE = 512
COL_CHUNK = 256
TAIL_BLOCKS = 2
POOL_HALO = 16
SCONV_HALO = 8
DCONV_HALO = 32
VMEM_LIMIT_BYTES = 56 * 1024 * 1024

_F32 = jnp.float32
_BF16 = jnp.bfloat16


def _dot(a, b):
    return jnp.dot(a, b, preferred_element_type=_F32)


def _layer_norm(v, g, b):
    mu = jnp.mean(v, axis=-1, keepdims=True)
    d = v - mu
    var = jnp.mean(d * d, axis=-1, keepdims=True)
    return d * jax.lax.rsqrt(var + LN_EPS) * g + b


def _silu(v):
    return v * jax.nn.sigmoid(v)


def _zero(ref):
    ref[...] = jnp.zeros(ref.shape, ref.dtype)


def _even_kernel(ts, tiles_per_seq, x_ref, win_ref, wout_ref, poolw_ref, pscale_ref, scw_ref,
                 scb_ref, g_ref, b_ref, o_ref, xa_buf, ch_buf, y_buf):
    tile_in_seq = pl.program_id(0) % tiles_per_seq

    @pl.when(tile_in_seq == 0)
    def _():
        _zero(xa_buf)
        _zero(ch_buf)

    x = x_ref[...]
    xb = x.astype(_BF16)
    pos = (tile_in_seq * ts + 1 + jax.lax.broadcasted_iota(jnp.int32, (ts, 1), 0)).astype(_F32)

    def proj(block, c0):
        lo = block * D_HALF + c0
        return _dot(xb, win_ref[:, lo:lo + COL_CHUNK])

    for g, w in enumerate(POOL_WINDOWS):
        c0 = g * POOL_GROUP
        cols = slice(c0, c0 + POOL_GROUP)
        xa = proj(0, c0)
        ga = proj(1, c0)
        wsum = jnp.concatenate([xa_buf[:, cols], xa], axis=0)
        xa_buf[:, cols] = xa[ts - POOL_HALO:ts]
        shift = 1
        while shift < w:
            wsum = wsum + pltpu.roll(wsum, shift, axis=0)
            shift *= 2
        pooled = wsum[POOL_HALO:] / jnp.minimum(pos, float(w)) - xa
        ya = _dot(pooled.astype(_BF16), poolw_ref[g]) * pscale_ref[:, cols] * _silu(ga)
        y_buf[:, cols] = ya.astype(_BF16)
    y = _dot(y_buf[:, 0:D_HALF], wout_ref[0:D_HALF, :])

    for c in range(D_HALF // COL_CHUNK):
        c0 = c * COL_CHUNK
        cols = slice(c0, c0 + COL_CHUNK)
        ch = proj(4, c0) * proj(2, c0)
        ext = jnp.concatenate([ch_buf[:, cols], ch], axis=0)
        ch_buf[:, cols] = ch[ts - SCONV_HALO:ts]
        conv = ext * scw_ref[0:1, cols]
        for k in range(1, SHORT_CONV):
            conv = ext * scw_ref[k:k + 1, cols] + pltpu.roll(conv, 1, axis=0)
        conv = conv[SCONV_HALO:] + scb_ref[:, cols]
        yb = proj(3, c0) * conv * _silu(proj(5, c0))
        y_buf[:, D_HALF + c0:D_HALF + c0 + COL_CHUNK] = yb.astype(_BF16)

    for r0 in range(0, ts, ts // TAIL_BLOCKS):
        rws = slice(r0, r0 + ts // TAIL_BLOCKS)
        yr = y[rws] + _dot(y_buf[rws, D_HALF:2 * D_HALF], wout_ref[D_HALF:2 * D_HALF, :])
        o_ref[rws, :] = _layer_norm(DEEPNORM_ALPHA * x[rws] + yr, g_ref[...], b_ref[...])


def _conv31_chunk(ext_ref, ext4_ref, dw_ref, cols):
    rows = ext_ref.shape[0] - DCONV_HALO + _SUBLANES
    half = _SUBLANES // 2
    conv = None
    for r in range(half - 1, -1, -1):
        part = None
        for b, src in enumerate((ext_ref, ext4_ref)):
            for q in range((CONF_CONV - 1 - half * b - r) // _SUBLANES + 1):
                d = _SUBLANES * q + half * b + r
                start = DCONV_HALO - _SUBLANES - _SUBLANES * q
                term = (src[start:start + rows, cols]
                        * dw_ref[CONF_CONV - 1 - d:CONF_CONV - d, cols])
                part = term if part is None else part + term
        conv = part if conv is None else part + pltpu.roll(conv, 1, axis=0)
    return conv


def _odd_kernel(ts, tiles_per_seq, x_ref, win_ref, wout_ref, sg_ref, sb_ref, sw_ref, sbias_ref,
                dw_ref, db_ref, dg_ref, dbeta_ref, g_ref, b_ref, o_ref,
                z_buf, z4_buf, cv_buf, v_buf, gate_buf, sgd_buf, y_buf):
    @pl.when(pl.program_id(0) % tiles_per_seq == 0)
    def _():
        z_buf[0:DCONV_HALO, :] = jnp.zeros((DCONV_HALO, D_HALF), _F32)

    x = x_ref[...]
    xb = x.astype(_BF16)

    def proj(block, c0):
        lo = block * D_HALF + c0
        return _dot(xb, win_ref[:, lo:lo + COL_CHUNK])

    for c in range(D_HALF // COL_CHUNK):
        c0 = c * COL_CHUNK
        cols = slice(c0, c0 + COL_CHUNK)
        z_buf[DCONV_HALO:, cols] = proj(3, c0) * jax.nn.sigmoid(proj(4, c0))
        z4_buf[:, cols] = pltpu.roll(z_buf[:, cols], _SUBLANES // 2, axis=0)
        cv_buf[:, cols] = (_conv31_chunk(z_buf, z4_buf, dw_ref, cols)[_SUBLANES:]
                           + db_ref[:, cols])
        z_buf[0:DCONV_HALO, cols] = z_buf[ts:ts + DCONV_HALO, cols]
    for c in range(D_HALF // COL_CHUNK):
        c0 = c * COL_CHUNK
        cols = slice(c0, c0 + COL_CHUNK)
        v_buf[:, cols] = proj(1, c0)
        gate_buf[:, cols] = proj(0, c0) * _silu(proj(2, c0))
        sgd_buf[:, cols] = _silu(proj(5, c0))

    vn = _layer_norm(v_buf[...], sg_ref[...], sb_ref[...]).astype(_BF16)
    row = jax.lax.broadcasted_iota(jnp.int32, (SGU_BLOCK, SGU_BLOCK), 0)
    col = jax.lax.broadcasted_iota(jnp.int32, (SGU_BLOCK, SGU_BLOCK), 1)
    mask = (col // SGU_CHUNK) <= (row // SGU_CHUNK)
    for hd in range(SGU_HEADS):
        c0 = hd * SGU_HEAD_DIM
        cols = slice(c0, c0 + SGU_HEAD_DIM)
        ws = jnp.where(mask, sw_ref[hd], 0.0).astype(_BF16)
        bias = sbias_ref[:, hd:hd + 1]
        for n in range(ts // SGU_BLOCK):
            rws = slice(n * SGU_BLOCK, (n + 1) * SGU_BLOCK)
            sv = _dot(ws, vn[rws, cols]) + bias
            y_buf[rws, cols] = (gate_buf[rws, cols] * sv).astype(_BF16)
    y = _dot(y_buf[...], wout_ref[0:D_HALF, :])

    for r0 in range(0, ts, ts // TAIL_BLOCKS):
        rws = slice(r0, r0 + ts // TAIL_BLOCKS)
        zn = _silu(_layer_norm(cv_buf[rws, :], dg_ref[...], dbeta_ref[...]))
        yd = (zn * sgd_buf[rws, :]).astype(_BF16)
        yr = y[rws] + _dot(yd, wout_ref[D_HALF:2 * D_HALF, :])
        o_ref[rws, :] = _layer_norm(DEEPNORM_ALPHA * x[rws] + yr, g_ref[...], b_ref[...])


def _layer_param(stacked, layer):
    tail = (0,) * (stacked.ndim - 1)
    return pl.BlockSpec((None,) + stacked.shape[1:], lambda s: (layer,) + tail,
                        pipeline_mode=pl.Buffered(1))


def _layer_call(body, ts, x2d, seq, params, scratch_shapes, name):
    tokens, d = x2d.shape
    tile_spec = pl.BlockSpec((ts, d), lambda s: (s, 0))
    return pl.pallas_call(
        functools.partial(body, ts, seq // ts),
        grid=(tokens // ts,),
        in_specs=[tile_spec] + [_layer_param(p, layer) for p, layer in params],
        out_specs=tile_spec,
        out_shape=jax.ShapeDtypeStruct(x2d.shape, x2d.dtype),
        scratch_shapes=scratch_shapes,
        compiler_params=pltpu.CompilerParams(
            dimension_semantics=("arbitrary",),
            vmem_limit_bytes=VMEM_LIMIT_BYTES),
        name=name,
    )(x2d, *[p for p, _ in params])


def _row(stacked):
    return stacked[:, None, :]


def kernel(x, ln_g, ln_b, w_in_even, w_out_even, pool_w, pool_scale, sconv_w, sconv_b,
           w_in_odd, w_out_odd, sgu_ln_g, sgu_ln_b, sgu_w, sgu_b,
           dconv_w, dconv_b, dnorm_g, dnorm_b):
    bsz, seq, d = x.shape
    assert seq % EVEN_TILE == 0 and seq % ODD_TILE == 0 and d == D_MODEL

    odd_f32 = pltpu.VMEM((ODD_TILE, D_HALF), _F32)
    even_scratch = [pltpu.VMEM((POOL_HALO, D_HALF), _F32), pltpu.VMEM((SCONV_HALO, D_HALF), _F32),
                    pltpu.VMEM((EVEN_TILE, 2 * D_HALF), _BF16)]
    odd_ext = pltpu.VMEM((DCONV_HALO + ODD_TILE, D_HALF), _F32)
    odd_scratch = [odd_ext, odd_ext, odd_f32, odd_f32, odd_f32, odd_f32,
                   pltpu.VMEM((ODD_TILE, D_HALF), _BF16)]

    w_in_even, w_out_even, pool_w, w_in_odd, w_out_odd = (
        w.astype(_BF16) for w in (w_in_even, w_out_even, pool_w, w_in_odd, w_out_odd))
    ln_g, ln_b = _row(ln_g), _row(ln_b)
    pool_scale, sconv_b = _row(pool_scale), _row(sconv_b)
    sgu_ln_g, sgu_ln_b = _row(sgu_ln_g), _row(sgu_ln_b)
    dconv_b, dnorm_g, dnorm_b = _row(dconv_b), _row(dnorm_g), _row(dnorm_b)
    sgu_bias = jnp.transpose(sgu_b, (0, 2, 1))

    h = x.reshape(bsz * seq, d)
    for layer in range(DEPTH):
        i = layer // 2
        if layer % 2 == 0:
            params = [(w_in_even, i), (w_out_even, i), (pool_w, i), (pool_scale, i),
                      (sconv_w, i), (sconv_b, i), (ln_g, layer), (ln_b, layer)]
            h = _layer_call(_even_kernel, EVEN_TILE, h, seq, params, even_scratch,
                            "pool_conv_layer")
        else:
            params = [(w_in_odd, i), (w_out_odd, i), (sgu_ln_g, i), (sgu_ln_b, i),
                      (sgu_w, i), (sgu_bias, i), (dconv_w, i), (dconv_b, i),
                      (dnorm_g, i), (dnorm_b, i), (ln_g, layer), (ln_b, layer)]
            h = _layer_call(_odd_kernel, ODD_TILE, h, seq, params, odd_scratch,
                            "sgu_conformer_layer")
    return h.reshape(bsz, seq, d)
```

```python
import functools

import jax
import jax.numpy as jnp
from jax.experimental import pallas as pl
from jax.experimental.pallas import tpu as pltpu

D_MODEL = 1024
D_HALF = 1024
POOL_WINDOWS = (2, 4, 8, 16)
POOL_GROUP = 256
SHORT_CONV = 3
SGU_BLOCK = 128
SGU_CHUNK = 64
SGU_HEADS = 4
SGU_HEAD_DIM = 256
CONF_CONV = 31
DEPTH = 4
DEEPNORM_ALPHA = (2 * DEPTH) ** 0.25
LN_EPS = 1e-5

_SUBLANES = 8
EVEN_TILE = 1024
ODD_TILE = 512
COL_CHUNK = 256
TAIL_BLOCKS = 4
POOL_HALO = 16
SCONV_HALO = 8
DCONV_HALO = 32
VMEM_LIMIT_BYTES = 56 * 1024 * 1024

_F32 = jnp.float32
_BF16 = jnp.bfloat16


def _dot(a, b):
    return jnp.dot(a, b, preferred_element_type=_F32)


def _layer_norm(v, g, b):
    mu = jnp.mean(v, axis=-1, keepdims=True)
    d = v - mu
    var = jnp.mean(d * d, axis=-1, keepdims=True)
    return d * jax.lax.rsqrt(var + LN_EPS) * g + b


def _silu(v):
    return v * jax.nn.sigmoid(v)


def _zero(ref):
    ref[...] = jnp.zeros(ref.shape, ref.dtype)


def _even_kernel(ts, tiles_per_seq, x_ref, win_ref, wout_ref, poolw_ref, pscale_ref, scw_ref,
                 scb_ref, g_ref, b_ref, o_ref, xa_buf, ch_buf, y_buf):
    tile_in_seq = pl.program_id(0) % tiles_per_seq

    @pl.when(tile_in_seq == 0)
    def _():
        _zero(xa_buf)
        _zero(ch_buf)

    x = x_ref[...]
    xb = x.astype(_BF16)
    pos = (tile_in_seq * ts + 1 + jax.lax.broadcasted_iota(jnp.int32, (ts, 1), 0)).astype(_F32)

    def proj(block, c0):
        lo = block * D_HALF + c0
        return _dot(xb, win_ref[:, lo:lo + COL_CHUNK])

    for g, w in enumerate(POOL_WINDOWS):
        c0 = g * POOL_GROUP
        cols = slice(c0, c0 + POOL_GROUP)
        xa = proj(0, c0)
        ga = proj(1, c0)
        wsum = jnp.concatenate([xa_buf[:, cols], xa], axis=0)
        xa_buf[:, cols] = xa[ts - POOL_HALO:ts]
        shift = 1
        while shift < w:
            wsum = wsum + pltpu.roll(wsum, shift, axis=0)
            shift *= 2
        pooled = wsum[POOL_HALO:] / jnp.minimum(pos, float(w)) - xa
        ya = _dot(pooled.astype(_BF16), poolw_ref[g]) * pscale_ref[:, cols] * _silu(ga)
        y_buf[:, cols] = ya.astype(_BF16)
    y = _dot(y_buf[:, 0:D_HALF], wout_ref[0:D_HALF, :])

    for c in range(D_HALF // COL_CHUNK):
        c0 = c * COL_CHUNK
        cols = slice(c0, c0 + COL_CHUNK)
        ch = proj(4, c0) * proj(2, c0)
        ext = jnp.concatenate([ch_buf[:, cols], ch], axis=0)
        ch_buf[:, cols] = ch[ts - SCONV_HALO:ts]
        conv = ext * scw_ref[0:1, cols]
        for k in range(1, SHORT_CONV):
            conv = ext * scw_ref[k:k + 1, cols] + pltpu.roll(conv, 1, axis=0)
        conv = conv[SCONV_HALO:] + scb_ref[:, cols]
        yb = proj(3, c0) * conv * _silu(proj(5, c0))
        y_buf[:, D_HALF + c0:D_HALF + c0 + COL_CHUNK] = yb.astype(_BF16)

    for r0 in range(0, ts, ts // TAIL_BLOCKS):
        rws = slice(r0, r0 + ts // TAIL_BLOCKS)
        yr = y[rws] + _dot(y_buf[rws, D_HALF:2 * D_HALF], wout_ref[D_HALF:2 * D_HALF, :])
        o_ref[rws, :] = _layer_norm(DEEPNORM_ALPHA * x[rws] + yr, g_ref[...], b_ref[...])


def _conv31_chunk(ext_ref, ext4_ref, dw_ref, cols):
    rows = ext_ref.shape[0] - DCONV_HALO + _SUBLANES
    half = _SUBLANES // 2
    conv = None
    for r in range(half - 1, -1, -1):
        part = None
        for b, src in enumerate((ext_ref, ext4_ref)):
            for q in range((CONF_CONV - 1 - half * b - r) // _SUBLANES + 1):
                d = _SUBLANES * q + half * b + r
                start = DCONV_HALO - _SUBLANES - _SUBLANES * q
                term = (src[start:start + rows, cols]
                        * dw_ref[CONF_CONV - 1 - d:CONF_CONV - d, cols])
                part = term if part is None else part + term
        conv = part if conv is None else part + pltpu.roll(conv, 1, axis=0)
    return conv


def _odd_kernel(ts, tiles_per_seq, x_ref, win_ref, wout_ref, sg_ref, sb_ref, sw_ref, sbias_ref,
                dw_ref, db_ref, dg_ref, dbeta_ref, g_ref, b_ref, o_ref,
                z_buf, z4_buf, cv_buf, v_buf, gate_buf, sgd_buf, y_buf):
    @pl.when(pl.program_id(0) % tiles_per_seq == 0)
    def _():
        z_buf[0:DCONV_HALO, :] = jnp.zeros((DCONV_HALO, D_HALF), _F32)

    x = x_ref[...]
    xb = x.astype(_BF16)

    def proj(block, c0):
        lo = block * D_HALF + c0
        return _dot(xb, win_ref[:, lo:lo + COL_CHUNK])

    for c in range(D_HALF // COL_CHUNK):
        c0 = c * COL_CHUNK
        cols = slice(c0, c0 + COL_CHUNK)
        z_buf[DCONV_HALO:, cols] = proj(3, c0) * jax.nn.sigmoid(proj(4, c0))
        z4_buf[:, cols] = pltpu.roll(z_buf[:, cols], _SUBLANES // 2, axis=0)
        cv_buf[:, cols] = (_conv31_chunk(z_buf, z4_buf, dw_ref, cols)[_SUBLANES:]
                           + db_ref[:, cols])
        z_buf[0:DCONV_HALO, cols] = z_buf[ts:ts + DCONV_HALO, cols]
    for c in range(D_HALF // COL_CHUNK):
        c0 = c * COL_CHUNK
        cols = slice(c0, c0 + COL_CHUNK)
        v_buf[:, cols] = proj(1, c0)
        gate_buf[:, cols] = proj(0, c0) * _silu(proj(2, c0))
        sgd_buf[:, cols] = _silu(proj(5, c0))

    vn = _layer_norm(v_buf[...], sg_ref[...], sb_ref[...]).astype(_BF16)
    row = jax.lax.broadcasted_iota(jnp.int32, (SGU_BLOCK, SGU_BLOCK), 0)
    col = jax.lax.broadcasted_iota(jnp.int32, (SGU_BLOCK, SGU_BLOCK), 1)
    mask = (col // SGU_CHUNK) <= (row // SGU_CHUNK)
    for hd in range(SGU_HEADS):
        c0 = hd * SGU_HEAD_DIM
        cols = slice(c0, c0 + SGU_HEAD_DIM)
        ws = jnp.where(mask, sw_ref[hd], 0.0).astype(_BF16)
        bias = sbias_ref[:, hd:hd + 1]
        for n in range(ts // SGU_BLOCK):
            rws = slice(n * SGU_BLOCK, (n + 1) * SGU_BLOCK)
            sv = _dot(ws, vn[rws, cols]) + bias
            y_buf[rws, cols] = (gate_buf[rws, cols] * sv).astype(_BF16)
    y = _dot(y_buf[...], wout_ref[0:D_HALF, :])

    for r0 in range(0, ts, ts // TAIL_BLOCKS):
        rws = slice(r0, r0 + ts // TAIL_BLOCKS)
        zn = _silu(_layer_norm(cv_buf[rws, :], dg_ref[...], dbeta_ref[...]))
        yd = (zn * sgd_buf[rws, :]).astype(_BF16)
        yr = y[rws] + _dot(yd, wout_ref[D_HALF:2 * D_HALF, :])
        o_ref[rws, :] = _layer_norm(DEEPNORM_ALPHA * x[rws] + yr, g_ref[...], b_ref[...])


def _layer_param(stacked, layer):
    tail = (0,) * (stacked.ndim - 1)
    return pl.BlockSpec((None,) + stacked.shape[1:], lambda s: (layer,) + tail,
                        pipeline_mode=pl.Buffered(1))


def _layer_call(body, ts, x2d, seq, params, scratch_shapes, name):
    tokens, d = x2d.shape
    tile_spec = pl.BlockSpec((ts, d), lambda s: (s, 0))
    return pl.pallas_call(
        functools.partial(body, ts, seq // ts),
        grid=(tokens // ts,),
        in_specs=[tile_spec] + [_layer_param(p, layer) for p, layer in params],
        out_specs=tile_spec,
        out_shape=jax.ShapeDtypeStruct(x2d.shape, x2d.dtype),
        scratch_shapes=scratch_shapes,
        compiler_params=pltpu.CompilerParams(
            dimension_semantics=("arbitrary",),
            vmem_limit_bytes=VMEM_LIMIT_BYTES),
        name=name,
    )(x2d, *[p for p, _ in params])


def _row(stacked):
    return stacked[:, None, :]


def kernel(x, ln_g, ln_b, w_in_even, w_out_even, pool_w, pool_scale, sconv_w, sconv_b,
           w_in_odd, w_out_odd, sgu_ln_g, sgu_ln_b, sgu_w, sgu_b,
           dconv_w, dconv_b, dnorm_g, dnorm_b):
    bsz, seq, d = x.shape
    assert seq % EVEN_TILE == 0 and seq % ODD_TILE == 0 and d == D_MODEL

    odd_f32 = pltpu.VMEM((ODD_TILE, D_HALF), _F32)
    even_scratch = [pltpu.VMEM((POOL_HALO, D_HALF), _F32), pltpu.VMEM((SCONV_HALO, D_HALF), _F32),
                    pltpu.VMEM((EVEN_TILE, 2 * D_HALF), _BF16)]
    odd_ext = pltpu.VMEM((DCONV_HALO + ODD_TILE, D_HALF), _F32)
    odd_scratch = [odd_ext, odd_ext, odd_f32, odd_f32, odd_f32, odd_f32,
                   pltpu.VMEM((ODD_TILE, D_HALF), _BF16)]

    w_in_even, w_out_even, pool_w, w_in_odd, w_out_odd = (
        w.astype(_BF16) for w in (w_in_even, w_out_even, pool_w, w_in_odd, w_out_odd))
    ln_g, ln_b = _row(ln_g), _row(ln_b)
    pool_scale, sconv_b = _row(pool_scale), _row(sconv_b)
    sgu_ln_g, sgu_ln_b = _row(sgu_ln_g), _row(sgu_ln_b)
    dconv_b, dnorm_g, dnorm_b = _row(dconv_b), _row(dnorm_g), _row(dnorm_b)
    sgu_bias = jnp.transpose(sgu_b, (0, 2, 1))

    h = x.reshape(bsz * seq, d)
    for layer in range(DEPTH):
        i = layer // 2
        if layer % 2 == 0:
            params = [(w_in_even, i), (w_out_even, i), (pool_w, i), (pool_scale, i),
                      (sconv_w, i), (sconv_b, i), (ln_g, layer), (ln_b, layer)]
            h = _layer_call(_even_kernel, EVEN_TILE, h, seq, params, even_scratch,
                            "pool_conv_layer")
        else:
            params = [(w_in_odd, i), (w_out_odd, i), (sgu_ln_g, i), (sgu_ln_b, i),
                      (sgu_w, i), (sgu_bias, i), (dconv_w, i), (dconv_b, i),
                      (dnorm_g, i), (dnorm_b, i), (ln_g, layer), (ln_b, layer)]
            h = _layer_call(_odd_kernel, ODD_TILE, h, seq, params, odd_scratch,
                            "sgu_conformer_layer")
    return h.reshape(bsz, seq, d)
```

```python
import functools

import jax
import jax.numpy as jnp
from jax.experimental import pallas as pl
from jax.experimental.pallas import tpu as pltpu

D_MODEL = 1024
D_HALF = 1024
POOL_WINDOWS = (2, 4, 8, 16)
POOL_GROUP = 256
SHORT_CONV = 3
SGU_BLOCK = 128
SGU_CHUNK = 64
SGU_HEADS = 4
SGU_HEAD_DIM = 256
CONF_CONV = 31
DEPTH = 4
DEEPNORM_ALPHA = (2 * DEPTH) ** 0.25
LN_EPS = 1e-5

_SUBLANES = 8
EVEN_TILE = 1024
ODD_TILE = 512
COL_CHUNK = 256
TAIL_BLOCKS = 4
POOL_HALO = 16
SCONV_HALO = 8
DCONV_HALO = 32
VMEM_LIMIT_BYTES = 56 * 1024 * 1024

_F32 = jnp.float32
_BF16 = jnp.bfloat16


def _dot(a, b):
    return jnp.dot(a, b, preferred_element_type=_F32)


def _layer_norm(v, g, b):
    mu = jnp.mean(v, axis=-1, keepdims=True)
    d = v - mu
    var = jnp.mean(d * d, axis=-1, keepdims=True)
    return d * jax.lax.rsqrt(var + LN_EPS) * g + b


def _silu(v):
    return v * jax.nn.sigmoid(v)


def _zero(ref):
    ref[...] = jnp.zeros(ref.shape, ref.dtype)


def _even_kernel(ts, tiles_per_seq, x_ref, win_ref, wout_ref, poolw_ref, pscale_ref, scw_ref,
                 scb_ref, g_ref, b_ref, o_ref, xa_buf, ch_buf, y_buf):
    tile_in_seq = pl.program_id(0) % tiles_per_seq

    @pl.when(tile_in_seq == 0)
    def _():
        _zero(xa_buf)
        _zero(ch_buf)

    x = x_ref[...]
    xb = x.astype(_BF16)
    pos = (tile_in_seq * ts + 1 + jax.lax.broadcasted_iota(jnp.int32, (ts, 1), 0)).astype(_F32)

    def proj(block, c0):
        lo = block * D_HALF + c0
        return _dot(xb, win_ref[:, lo:lo + COL_CHUNK])

    for g, w in enumerate(POOL_WINDOWS):
        c0 = g * POOL_GROUP
        cols = slice(c0, c0 + POOL_GROUP)
        xa = proj(0, c0)
        ga = proj(1, c0)
        wsum = jnp.concatenate([xa_buf[:, cols], xa], axis=0)
        xa_buf[:, cols] = xa[ts - POOL_HALO:ts]
        shift = 1
        while shift < w:
            wsum = wsum + pltpu.roll(wsum, shift, axis=0)
            shift *= 2
        pooled = wsum[POOL_HALO:] / jnp.minimum(pos, float(w)) - xa
        ya = _dot(pooled.astype(_BF16), poolw_ref[g]) * pscale_ref[:, cols] * _silu(ga)
        y_buf[:, cols] = ya.astype(_BF16)
    y = _dot(y_buf[:, 0:D_HALF], wout_ref[0:D_HALF, :])

    for c in range(D_HALF // COL_CHUNK):
        c0 = c * COL_CHUNK
        cols = slice(c0, c0 + COL_CHUNK)
        ch = proj(4, c0) * proj(2, c0)
        ext = jnp.concatenate([ch_buf[:, cols], ch], axis=0)
        ch_buf[:, cols] = ch[ts - SCONV_HALO:ts]
        conv = ext * scw_ref[0:1, cols]
        for k in range(1, SHORT_CONV):
            conv = ext * scw_ref[k:k + 1, cols] + pltpu.roll(conv, 1, axis=0)
        conv = conv[SCONV_HALO:] + scb_ref[:, cols]
        yb = proj(3, c0) * conv * _silu(proj(5, c0))
        y_buf[:, D_HALF + c0:D_HALF + c0 + COL_CHUNK] = yb.astype(_BF16)

    for r0 in range(0, ts, ts // TAIL_BLOCKS):
        rws = slice(r0, r0 + ts // TAIL_BLOCKS)
        yr = y[rws] + _dot(y_buf[rws, D_HALF:2 * D_HALF], wout_ref[D_HALF:2 * D_HALF, :])
        o_ref[rws, :] = _layer_norm(DEEPNORM_ALPHA * x[rws] + yr, g_ref[...], b_ref[...])


def _conv31_chunk(ext_ref, ext4_ref, dw_ref, cols):
    rows = ext_ref.shape[0] - DCONV_HALO + _SUBLANES
    half = _SUBLANES // 2
    conv = None
    for r in range(half - 1, -1, -1):
        part = None
        for b, src in enumerate((ext_ref, ext4_ref)):
            for q in range((CONF_CONV - 1 - half * b - r) // _SUBLANES + 1):
                d = _SUBLANES * q + half * b + r
                start = DCONV_HALO - _SUBLANES - _SUBLANES * q
                term = (src[start:start + rows, cols]
                        * dw_ref[CONF_CONV - 1 - d:CONF_CONV - d, cols])
                part = term if part is None else part + term
        conv = part if conv is None else part + pltpu.roll(conv, 1, axis=0)
    return conv


def _odd_kernel(ts, tiles_per_seq, x_ref, win_ref, wout_ref, sg_ref, sb_ref, sw_ref, sbias_ref,
                dw_ref, db_ref, dg_ref, dbeta_ref, g_ref, b_ref, o_ref,
                z_buf, z4_buf, cv_buf, v_buf, gate_buf, sgd_buf, y_buf):
    @pl.when(pl.program_id(0) % tiles_per_seq == 0)
    def _():
        z_buf[0:DCONV_HALO, :] = jnp.zeros((DCONV_HALO, D_HALF), _F32)

    x = x_ref[...]
    xb = x.astype(_BF16)

    def proj(block, c0):
        lo = block * D_HALF + c0
        return _dot(xb, win_ref[:, lo:lo + COL_CHUNK])

    for c in range(D_HALF // COL_CHUNK):
        c0 = c * COL_CHUNK
        cols = slice(c0, c0 + COL_CHUNK)
        glu_gate = jax.nn.sigmoid(proj(4, c0))
        z_buf[DCONV_HALO:, cols] = proj(3, c0) * glu_gate
        z4_buf[:, cols] = pltpu.roll(z_buf[:, cols], _SUBLANES // 2, axis=0)
        cv_buf[:, cols] = (_conv31_chunk(z_buf, z4_buf, dw_ref, cols)[_SUBLANES:]
                           + db_ref[:, cols])
        z_buf[0:DCONV_HALO, cols] = z_buf[ts:ts + DCONV_HALO, cols]
    for c in range(D_HALF // COL_CHUNK):
        c0 = c * COL_CHUNK
        cols = slice(c0, c0 + COL_CHUNK)
        v_buf[:, cols] = proj(1, c0)
        swish = _silu(proj(2, c0))
        gate_buf[:, cols] = proj(0, c0) * swish
        sgd_buf[:, cols] = _silu(proj(5, c0))

    vn = _layer_norm(v_buf[...], sg_ref[...], sb_ref[...]).astype(_BF16)
    row = jax.lax.broadcasted_iota(jnp.int32, (SGU_BLOCK, SGU_BLOCK), 0)
    col = jax.lax.broadcasted_iota(jnp.int32, (SGU_BLOCK, SGU_BLOCK), 1)
    mask = (col // SGU_CHUNK) <= (row // SGU_CHUNK)
    for hd in range(SGU_HEADS):
        c0 = hd * SGU_HEAD_DIM
        cols = slice(c0, c0 + SGU_HEAD_DIM)
        ws = jnp.where(mask, sw_ref[hd], 0.0).astype(_BF16)
        bias = sbias_ref[:, hd:hd + 1]
        for n in range(ts // SGU_BLOCK):
            rws = slice(n * SGU_BLOCK, (n + 1) * SGU_BLOCK)
            sv = _dot(ws, vn[rws, cols]) + bias
            y_buf[rws, cols] = (gate_buf[rws, cols] * sv).astype(_BF16)
    y = _dot(y_buf[...], wout_ref[0:D_HALF, :])

    for r0 in range(0, ts, ts // TAIL_BLOCKS):
        rws = slice(r0, r0 + ts // TAIL_BLOCKS)
        zn = _silu(_layer_norm(cv_buf[rws, :], dg_ref[...], dbeta_ref[...]))
        yd = (zn * sgd_buf[rws, :]).astype(_BF16)
        yr = y[rws] + _dot(yd, wout_ref[D_HALF:2 * D_HALF, :])
        o_ref[rws, :] = _layer_norm(DEEPNORM_ALPHA * x[rws] + yr, g_ref[...], b_ref[...])


def _layer_param(stacked, layer):
    tail = (0,) * (stacked.ndim - 1)
    return pl.BlockSpec((None,) + stacked.shape[1:], lambda s: (layer,) + tail,
                        pipeline_mode=pl.Buffered(1))


def _layer_call(body, ts, x2d, seq, params, scratch_shapes, name):
    tokens, d = x2d.shape
    tile_spec = pl.BlockSpec((ts, d), lambda s: (s, 0))
    return pl.pallas_call(
        functools.partial(body, ts, seq // ts),
        grid=(tokens // ts,),
        in_specs=[tile_spec] + [_layer_param(p, layer) for p, layer in params],
        out_specs=tile_spec,
        out_shape=jax.ShapeDtypeStruct(x2d.shape, x2d.dtype),
        scratch_shapes=scratch_shapes,
        compiler_params=pltpu.CompilerParams(
            dimension_semantics=("arbitrary",),
            vmem_limit_bytes=VMEM_LIMIT_BYTES),
        name=name,
    )(x2d, *[p for p, _ in params])


def _row(stacked):
    return stacked[:, None, :]


def kernel(x, ln_g, ln_b, w_in_even, w_out_even, pool_w, pool_scale, sconv_w, sconv_b,
           w_in_odd, w_out_odd, sgu_ln_g, sgu_ln_b, sgu_w, sgu_b,
           dconv_w, dconv_b, dnorm_g, dnorm_b):
    bsz, seq, d = x.shape
    assert seq % EVEN_TILE == 0 and seq % ODD_TILE == 0 and d == D_MODEL

    odd_f32 = pltpu.VMEM((ODD_TILE, D_HALF), _F32)
    even_scratch = [pltpu.VMEM((POOL_HALO, D_HALF), _F32), pltpu.VMEM((SCONV_HALO, D_HALF), _F32),
                    pltpu.VMEM((EVEN_TILE, 2 * D_HALF), _BF16)]
    odd_ext = pltpu.VMEM((DCONV_HALO + ODD_TILE, D_HALF), _F32)
    odd_scratch = [odd_ext, odd_ext, odd_f32, odd_f32, odd_f32, odd_f32,
                   pltpu.VMEM((ODD_TILE, D_HALF), _BF16)]

    w_in_even, w_out_even, pool_w, w_in_odd, w_out_odd = (
        w.astype(_BF16) for w in (w_in_even, w_out_even, pool_w, w_in_odd, w_out_odd))
    ln_g, ln_b = _row(ln_g), _row(ln_b)
    pool_scale, sconv_b = _row(pool_scale), _row(sconv_b)
    sgu_ln_g, sgu_ln_b = _row(sgu_ln_g), _row(sgu_ln_b)
    dconv_b, dnorm_g, dnorm_b = _row(dconv_b), _row(dnorm_g), _row(dnorm_b)
    sgu_bias = jnp.transpose(sgu_b, (0, 2, 1))

    h = x.reshape(bsz * seq, d)
    for layer in range(DEPTH):
        i = layer // 2
        if layer % 2 == 0:
            params = [(w_in_even, i), (w_out_even, i), (pool_w, i), (pool_scale, i),
                      (sconv_w, i), (sconv_b, i), (ln_g, layer), (ln_b, layer)]
            h = _layer_call(_even_kernel, EVEN_TILE, h, seq, params, even_scratch,
                            "pool_conv_layer")
        else:
            params = [(w_in_odd, i), (w_out_odd, i), (sgu_ln_g, i), (sgu_ln_b, i),
                      (sgu_w, i), (sgu_bias, i), (dconv_w, i), (dconv_b, i),
                      (dnorm_g, i), (dnorm_b, i), (ln_g, layer), (ln_b, layer)]
            h = _layer_call(_odd_kernel, ODD_TILE, h, seq, params, odd_scratch,
                            "sgu_conformer_layer")
    return h.reshape(bsz, seq, d)
```

```python
import functools

import jax
import jax.numpy as jnp
from jax.experimental import pallas as pl
from jax.experimental.pallas import tpu as pltpu

D_MODEL = 1024
D_HALF = 1024
POOL_WINDOWS = (2, 4, 8, 16)
POOL_GROUP = 256
SHORT_CONV = 3
SGU_BLOCK = 128
SGU_CHUNK = 64
SGU_HEADS = 4
SGU_HEAD_DIM = 256
CONF_CONV = 31
DEPTH = 4
DEEPNORM_ALPHA = (2 * DEPTH) ** 0.25
LN_EPS = 1e-5

_SUBLANES = 8
EVEN_TILE = 1024
ODD_TILE = 512
COL_CHUNK = 256
TAIL_BLOCKS = 4
POOL_HALO = 16
SCONV_HALO = 8
DCONV_HALO = 32
VMEM_LIMIT_BYTES = 56 * 1024 * 1024

_F32 = jnp.float32
_BF16 = jnp.bfloat16


def _dot(a, b):
    return jnp.dot(a, b, preferred_element_type=_F32)


def _layer_norm(v, g, b):
    mu = jnp.mean(v, axis=-1, keepdims=True)
    d = v - mu
    var = jnp.mean(d * d, axis=-1, keepdims=True)
    return d * jax.lax.rsqrt(var + LN_EPS) * g + b


def _silu(v):
    return v * jax.nn.sigmoid(v)


def _zero(ref):
    ref[...] = jnp.zeros(ref.shape, ref.dtype)


def _even_kernel(ts, tiles_per_seq, x_ref, win_ref, wout_ref, poolw_ref, pscale_ref, scw_ref,
                 scb_ref, g_ref, b_ref, o_ref, xa_buf, ch_buf, y_buf):
    tile_in_seq = pl.program_id(0) % tiles_per_seq

    @pl.when(tile_in_seq == 0)
    def _():
        _zero(xa_buf)
        _zero(ch_buf)

    x = x_ref[...]
    xb = x.astype(_BF16)
    pos = (tile_in_seq * ts + 1 + jax.lax.broadcasted_iota(jnp.int32, (ts, 1), 0)).astype(_F32)

    def proj(block, c0):
        lo = block * D_HALF + c0
        return _dot(xb, win_ref[:, lo:lo + COL_CHUNK])

    for g, w in enumerate(POOL_WINDOWS):
        c0 = g * POOL_GROUP
        cols = slice(c0, c0 + POOL_GROUP)
        xa = proj(0, c0)
        ga = proj(1, c0)
        wsum = jnp.concatenate([xa_buf[:, cols], xa], axis=0)
        xa_buf[:, cols] = xa[ts - POOL_HALO:ts]
        shift = 1
        while shift < w:
            wsum = wsum + pltpu.roll(wsum, shift, axis=0)
            shift *= 2
        pooled = wsum[POOL_HALO:] / jnp.minimum(pos, float(w)) - xa
        ya = _dot(pooled.astype(_BF16), poolw_ref[g]) * pscale_ref[:, cols] * _silu(ga)
        y_buf[:, cols] = ya.astype(_BF16)
    y = _dot(y_buf[:, 0:D_HALF], wout_ref[0:D_HALF, :])

    for c in range(D_HALF // COL_CHUNK):
        c0 = c * COL_CHUNK
        cols = slice(c0, c0 + COL_CHUNK)
        ch = proj(4, c0) * proj(2, c0)
        ext = jnp.concatenate([ch_buf[:, cols], ch], axis=0)
        ch_buf[:, cols] = ch[ts - SCONV_HALO:ts]
        conv = ext * scw_ref[0:1, cols]
        for k in range(1, SHORT_CONV):
            conv = ext * scw_ref[k:k + 1, cols] + pltpu.roll(conv, 1, axis=0)
        conv = conv[SCONV_HALO:] + scb_ref[:, cols]
        yb = proj(3, c0) * conv * _silu(proj(5, c0))
        y_buf[:, D_HALF + c0:D_HALF + c0 + COL_CHUNK] = yb.astype(_BF16)

    for r0 in range(0, ts, ts // TAIL_BLOCKS):
        rws = slice(r0, r0 + ts // TAIL_BLOCKS)
        yr = y[rws] + _dot(y_buf[rws, D_HALF:2 * D_HALF], wout_ref[D_HALF:2 * D_HALF, :])
        o_ref[rws, :] = _layer_norm(DEEPNORM_ALPHA * x[rws] + yr, g_ref[...], b_ref[...])


def _conv31_chunk(ext_ref, ext4_ref, dw_ref, cols):
    rows = ext_ref.shape[0] - DCONV_HALO + _SUBLANES
    half = _SUBLANES // 2
    conv = None
    for r in range(half - 1, -1, -1):
        part = None
        for b, src in enumerate((ext_ref, ext4_ref)):
            for q in range((CONF_CONV - 1 - half * b - r) // _SUBLANES + 1):
                d = _SUBLANES * q + half * b + r
                start = DCONV_HALO - _SUBLANES - _SUBLANES * q
                term = (src[start:start + rows, cols]
                        * dw_ref[CONF_CONV - 1 - d:CONF_CONV - d, cols])
                part = term if part is None else part + term
        conv = part if conv is None else part + pltpu.roll(conv, 1, axis=0)
    return conv


def _odd_kernel(ts, tiles_per_seq, x_ref, win_ref, wout_ref, sg_ref, sb_ref, sw_ref, sbias_ref,
                dw_ref, db_ref, dg_ref, dbeta_ref, g_ref, b_ref, o_ref,
                z_buf, z4_buf, cv_buf, v_buf, gate_buf, sgd_buf, y_buf):
    @pl.when(pl.program_id(0) % tiles_per_seq == 0)
    def _():
        z_buf[0:DCONV_HALO, :] = jnp.zeros((DCONV_HALO, D_HALF), _F32)

    x = x_ref[...]
    xb = x.astype(_BF16)

    def proj(block, c0):
        lo = block * D_HALF + c0
        return _dot(xb, win_ref[:, lo:lo + COL_CHUNK])

    for c in range(D_HALF // COL_CHUNK):
        c0 = c * COL_CHUNK
        cols = slice(c0, c0 + COL_CHUNK)
        glu_gate = jax.nn.sigmoid(proj(4, c0))
        z_buf[DCONV_HALO:, cols] = proj(3, c0) * glu_gate
        z4_buf[:, cols] = pltpu.roll(z_buf[:, cols], _SUBLANES // 2, axis=0)
        cv_buf[:, cols] = (_conv31_chunk(z_buf, z4_buf, dw_ref, cols)[_SUBLANES:]
                           + db_ref[:, cols])
        z_buf[0:DCONV_HALO, cols] = z_buf[ts:ts + DCONV_HALO, cols]
    for c in range(D_HALF // COL_CHUNK):
        c0 = c * COL_CHUNK
        cols = slice(c0, c0 + COL_CHUNK)
        sgd_buf[:, cols] = _silu(proj(5, c0))
        v_buf[:, cols] = proj(1, c0)
        swish = _silu(proj(2, c0))
        gate_buf[:, cols] = proj(0, c0) * swish

    vn = _layer_norm(v_buf[...], sg_ref[...], sb_ref[...]).astype(_BF16)
    row = jax.lax.broadcasted_iota(jnp.int32, (SGU_BLOCK, SGU_BLOCK), 0)
    col = jax.lax.broadcasted_iota(jnp.int32, (SGU_BLOCK, SGU_BLOCK), 1)
    mask = (col // SGU_CHUNK) <= (row // SGU_CHUNK)
    for hd in range(SGU_HEADS):
        c0 = hd * SGU_HEAD_DIM
        cols = slice(c0, c0 + SGU_HEAD_DIM)
        ws = jnp.where(mask, sw_ref[hd], 0.0).astype(_BF16)
        bias = sbias_ref[:, hd:hd + 1]
        for n in range(ts // SGU_BLOCK):
            rws = slice(n * SGU_BLOCK, (n + 1) * SGU_BLOCK)
            sv = _dot(ws, vn[rws, cols]) + bias
            y_buf[rws, cols] = (gate_buf[rws, cols] * sv).astype(_BF16)
    y = _dot(y_buf[...], wout_ref[0:D_HALF, :])

    for r0 in range(0, ts, ts // TAIL_BLOCKS):
        rws = slice(r0, r0 + ts // TAIL_BLOCKS)
        zn = _silu(_layer_norm(cv_buf[rws, :], dg_ref[...], dbeta_ref[...]))
        yd = (zn * sgd_buf[rws, :]).astype(_BF16)
        yr = y[rws] + _dot(yd, wout_ref[D_HALF:2 * D_HALF, :])
        o_ref[rws, :] = _layer_norm(DEEPNORM_ALPHA * x[rws] + yr, g_ref[...], b_ref[...])


def _layer_param(stacked, layer):
    tail = (0,) * (stacked.ndim - 1)
    return pl.BlockSpec((None,) + stacked.shape[1:], lambda s: (layer,) + tail,
                        pipeline_mode=pl.Buffered(1))


def _layer_call(body, ts, x2d, seq, params, scratch_shapes, name):
    tokens, d = x2d.shape
    tile_spec = pl.BlockSpec((ts, d), lambda s: (s, 0))
    return pl.pallas_call(
        functools.partial(body, ts, seq // ts),
        grid=(tokens // ts,),
        in_specs=[tile_spec] + [_layer_param(p, layer) for p, layer in params],
        out_specs=tile_spec,
        out_shape=jax.ShapeDtypeStruct(x2d.shape, x2d.dtype),
        scratch_shapes=scratch_shapes,
        compiler_params=pltpu.CompilerParams(
            dimension_semantics=("arbitrary",),
            vmem_limit_bytes=VMEM_LIMIT_BYTES),
        name=name,
    )(x2d, *[p for p, _ in params])


def _row(stacked):
    return stacked[:, None, :]


def kernel(x, ln_g, ln_b, w_in_even, w_out_even, pool_w, pool_scale, sconv_w, sconv_b,
           w_in_odd, w_out_odd, sgu_ln_g, sgu_ln_b, sgu_w, sgu_b,
           dconv_w, dconv_b, dnorm_g, dnorm_b):
    bsz, seq, d = x.shape
    assert seq % EVEN_TILE == 0 and seq % ODD_TILE == 0 and d == D_MODEL

    odd_f32 = pltpu.VMEM((ODD_TILE, D_HALF), _F32)
    even_scratch = [pltpu.VMEM((POOL_HALO, D_HALF), _F32), pltpu.VMEM((SCONV_HALO, D_HALF), _F32),
                    pltpu.VMEM((EVEN_TILE, 2 * D_HALF), _BF16)]
    odd_ext = pltpu.VMEM((DCONV_HALO + ODD_TILE, D_HALF), _F32)
    odd_scratch = [odd_ext, odd_ext, odd_f32, odd_f32, odd_f32, odd_f32,
                   pltpu.VMEM((ODD_TILE, D_HALF), _BF16)]

    w_in_even, w_out_even, pool_w, w_in_odd, w_out_odd = (
        w.astype(_BF16) for w in (w_in_even, w_out_even, pool_w, w_in_odd, w_out_odd))
    ln_g, ln_b = _row(ln_g), _row(ln_b)
    pool_scale, sconv_b = _row(pool_scale), _row(sconv_b)
    sgu_ln_g, sgu_ln_b = _row(sgu_ln_g), _row(sgu_ln_b)
    dconv_b, dnorm_g, dnorm_b = _row(dconv_b), _row(dnorm_g), _row(dnorm_b)
    sgu_bias = jnp.transpose(sgu_b, (0, 2, 1))

    h = x.reshape(bsz * seq, d)
    for layer in range(DEPTH):
        i = layer // 2
        if layer % 2 == 0:
            params = [(w_in_even, i), (w_out_even, i), (pool_w, i), (pool_scale, i),
                      (sconv_w, i), (sconv_b, i), (ln_g, layer), (ln_b, layer)]
            h = _layer_call(_even_kernel, EVEN_TILE, h, seq, params, even_scratch,
                            "pool_conv_layer")
        else:
            params = [(w_in_odd, i), (w_out_odd, i), (sgu_ln_g, i), (sgu_ln_b, i),
                      (sgu_w, i), (sgu_bias, i), (dconv_w, i), (dconv_b, i),
                      (dnorm_g, i), (dnorm_b, i), (ln_g, layer), (ln_b, layer)]
            h = _layer_call(_odd_kernel, ODD_TILE, h, seq, params, odd_scratch,
                            "sgu_conformer_layer")
    return h.reshape(bsz, seq, d)
```
